```python
import functools
import jax, jax.numpy as jnp
from jax import lax
import numpy as np

D_MODEL = 1024
BATCH = 2
SEQ = 8192
DEPTH = 2
DEC_BATCH = 32
DEC_SEQ = 4
PAST_LEN = 16384
PAGE_SIZE = 128

N_A_LAYERS = DEPTH // 2
N_B_LAYERS = DEPTH - N_A_LAYERS
RET_HEADS = 4
RET_DK = D_MODEL // RET_HEADS
RET_DV = 2 * D_MODEL // RET_HEADS
RET_CHUNK = 128
ROPE_BASE = 10000.0
SB_HEADS = 8
SB_HD = D_MODEL // SB_HEADS
SB_BLOCK = 128
SB_SCALE = SB_HD ** -0.5
SB_BIAS_INIT = -6.0
N_GROUPS = 4
EXPERTS_PER_GROUP = 8
N_EXPERTS = N_GROUPS * EXPERTS_PER_GROUP
TOP_K = 2
D_EXPERT = D_MODEL // 2
MOE_BLOCK = 128
EPS = 1e-6

kernel_name = 'yoco_retention_stickbreaking_hmoe_step'


def rmsnorm(x, g):
    xf = x.astype(jnp.float32)
    y = xf * lax.rsqrt(jnp.mean(xf * xf, axis=-1, keepdims=True) + EPS) * g.astype(jnp.float32)
    return y.astype(x.dtype)


def rotary(x, pos):
    half = x.shape[-1] // 2
    inv = ROPE_BASE ** (-jnp.arange(half, dtype=jnp.float32) / half)
    ang = pos.astype(jnp.float32)[:, None] * inv[None, :]
    cos = jnp.cos(ang)[None, :, None, :]
    sin = jnp.sin(ang)[None, :, None, :]
    xf = x.astype(jnp.float32)
    x1, x2 = xf[..., :half], xf[..., half:]
    return jnp.concatenate([x1 * cos - x2 * sin, x1 * sin + x2 * cos], axis=-1)


def retention_chunkwise(q, k, v, s0, log_gamma, chunk):
    B, T, H, dk = q.shape
    dv = v.shape[-1]
    nc = T // chunk
    def to_chunks(a):
        return jnp.moveaxis(a.reshape(B, nc, chunk, H, a.shape[-1]), 1, 0)
    qc, kc, vc = to_chunks(q), to_chunks(k), to_chunks(v.astype(jnp.float32))
    idx = jnp.arange(chunk, dtype=jnp.float32)
    rel = idx[:, None] - idx[None, :]
    causal = rel >= 0
    dmask = jnp.where(causal, jnp.exp(jnp.where(causal, rel, 0.0)[None] * log_gamma[:, None, None]), 0.0)
    q_dec = jnp.exp((idx + 1.0)[:, None] * log_gamma[None, :])
    k_dec = jnp.exp((chunk - 1.0 - idx)[:, None] * log_gamma[None, :])
    c_dec = jnp.exp(chunk * log_gamma)

    def step(S, inp):
        qi, ki, vi = inp
        att = jnp.einsum('bihd,bjhd->bhij', qi, ki) * dmask[None]
        o = jnp.einsum('bhij,bjhe->bihe', att, vi) + jnp.einsum('bihd,bhde->bihe', qi, S) * q_dec[None, :, :, None]
        S = S * c_dec[None, :, None, None] + jnp.einsum('bjhd,bjhe->bhde', ki * k_dec[None, :, :, None], vi)
        return S, o

    S, o = lax.scan(step, s0.astype(jnp.float32), (qc, kc, vc))
    o = jnp.moveaxis(o, 0, 1).reshape(B, T, H, dv)
    return o, S


def retention_layer(h, pos, s0, w_in, w_o, log_gamma, chunk):
    B, T, _ = h.shape
    qk_w = RET_HEADS * RET_DK
    v_w = RET_HEADS * RET_DV
    proj = h @ w_in
    q, k, v, g = jnp.split(proj, [qk_w, 2 * qk_w, 2 * qk_w + v_w], axis=-1)
    q = rotary(q.reshape(B, T, RET_HEADS, RET_DK), pos)
    k = rotary(k.reshape(B, T, RET_HEADS, RET_DK), pos) * (RET_DK ** -0.5)
    v = v.reshape(B, T, RET_HEADS, RET_DV)
    o, S = retention_chunkwise(q, k, v, s0, log_gamma, chunk)
    o = o * lax.rsqrt(jnp.mean(o * o, axis=-1, keepdims=True) + EPS)
    o = o.reshape(B, T, v_w).astype(h.dtype) * jax.nn.silu(g)
    return o @ w_o, S.astype(s0.dtype)


def sb_update(carry, qb, kb, vb, mask, bias):
    acc, logc = carry
    z = jnp.einsum('bhqd,bhkd->bhqk', qb.astype(jnp.float32), kb.astype(jnp.float32)) * SB_SCALE
    z = z + bias.astype(jnp.float32)[None, :, None, None]
    log_keep = jnp.where(mask, jax.nn.log_sigmoid(-z), 0.0)
    later = lax.cumsum(log_keep, axis=3, reverse=True) - log_keep
    log_a = jax.nn.log_sigmoid(z) + later + logc[..., None]
    a = jnp.where(mask, jnp.exp(log_a), 0.0)
    acc = acc + jnp.einsum('bhqk,bhkd->bhqd', a, vb.astype(jnp.float32))
    logc = logc + jnp.sum(log_keep, axis=-1)
    return acc, logc


def sb_prompt(q, k, v, bias):
    B, T, H, d = q.shape
    nb = T // SB_BLOCK
    qh, kh, vh = (jnp.swapaxes(a, 1, 2) for a in (q, k, v))
    ar = jnp.arange(SB_BLOCK)

    def q_block(i):
        qb = lax.dynamic_slice_in_dim(qh, i * SB_BLOCK, SB_BLOCK, axis=2)
        qpos = i * SB_BLOCK + ar
        def body(j, carry):
            kbi = i - j
            kb = lax.dynamic_slice_in_dim(kh, kbi * SB_BLOCK, SB_BLOCK, axis=2)
            vb = lax.dynamic_slice_in_dim(vh, kbi * SB_BLOCK, SB_BLOCK, axis=2)
            mask = (kbi * SB_BLOCK + ar)[None, :] < qpos[:, None]
            return sb_update(carry, qb, kb, vb, mask, bias)
        init = (jnp.zeros((B, H, SB_BLOCK, d), jnp.float32), jnp.zeros((B, H, SB_BLOCK), jnp.float32))
        acc, _ = lax.fori_loop(0, i + 1, body, init)
        return acc

    out = lax.map(q_block, jnp.arange(nb))
    out = jnp.transpose(out, (1, 0, 3, 2, 4)).reshape(B, T, H, d)
    return out.astype(q.dtype)


def sb_sample(q, k, v, bias, k_cache, v_cache, page_table):
    B, T, H, d = q.shape
    qh, kh, vh = (jnp.swapaxes(a, 1, 2) for a in (q, k, v))
    ar = jnp.arange(T)
    carry = (jnp.zeros((B, H, T, d), jnp.float32), jnp.zeros((B, H, T), jnp.float32))
    carry = sb_update(carry, qh, kh, vh, ar[None, :] < ar[:, None], bias)
    mask_past = jnp.ones((T, PAGE_SIZE), dtype=bool)

    def body(c, phys):
        kb = jnp.swapaxes(k_cache[phys], 1, 2)
        vb = jnp.swapaxes(v_cache[phys], 1, 2)
        return sb_update(c, qh, kb, vb, mask_past, bias), None

    (acc, _), _ = lax.scan(body, carry, page_table[:, ::-1].T)
    return jnp.swapaxes(acc, 1, 2).astype(q.dtype)


def grouped_experts(h, expert_id, gate, w1, w3, w2):
    N, D = h.shape
    n_assign = N * TOP_K
    blk = max(1, min(MOE_BLOCK, n_assign // N_EXPERTS))
    eid = expert_id.reshape(-1)
    tok = jnp.repeat(jnp.arange(N), TOP_K)
    gt = gate.reshape(-1)
    order = jnp.argsort(eid)
    eid_s, tok_s, gt_s = eid[order], tok[order], gt[order]
    counts = jnp.bincount(eid, length=N_EXPERTS)
    start = jnp.cumsum(counts) - counts
    padded = ((counts + blk - 1) // blk) * blk
    pstart = jnp.cumsum(padded) - padded
    dest = pstart[eid_s] + (jnp.arange(n_assign) - start[eid_s])
    n_blocks = -(-n_assign // blk) + N_EXPERTS
    xpad = jnp.zeros((n_blocks * blk, D), h.dtype).at[dest].set(h[tok_s])
    block_e = jnp.minimum(jnp.searchsorted(pstart + padded, jnp.arange(n_blocks) * blk, side='right'), N_EXPERTS - 1)

    def run(args):
        xb, e = args
        return (jax.nn.silu(xb @ w1[e]) * (xb @ w3[e])) @ w2[e]

    ypad = lax.map(run, (xpad.reshape(n_blocks, blk, D), block_e)).reshape(n_blocks * blk, D)
    y = jnp.zeros((N, D), jnp.float32).at[tok_s].add(ypad[dest].astype(jnp.float32) * gt_s[:, None])
    return y.astype(h.dtype)


def hier_moe(h, wg, bg, we, be, w1, w3, w2):
    N = h.shape[0]
    hf = h.astype(jnp.float32)
    g_logits = hf @ wg.astype(jnp.float32) + bg.astype(jnp.float32)
    g_prob = jax.nn.softmax(g_logits, axis=-1)
    g_sel = jnp.argmax(g_logits, axis=-1)
    rows = jnp.arange(N)
    g_w = g_prob[rows, g_sel]
    e_logits = (hf @ we.astype(jnp.float32) + be.astype(jnp.float32)).reshape(N, N_GROUPS, EXPERTS_PER_GROUP)
    e_sel_logits = e_logits[rows, g_sel]
    top_v, top_i = lax.top_k(e_sel_logits, TOP_K)
    gate = g_w[:, None] * jax.nn.softmax(top_v, axis=-1)
    expert_id = g_sel[:, None] * EXPERTS_PER_GROUP + top_i
    return grouped_experts(h, expert_id, gate, w1, w3, w2)


def trunk(x, pos, ret_s0, sb_fn, g_mix, g_ffn, g_kv, g_final, ret_w_in, ret_wo, kv_w, sb_wq, sb_wo, sb_bias,
          moe_wg, moe_bg, moe_we, moe_be, moe_w1, moe_w3, moe_w2):
    B, T, D = x.shape
    log_gamma = jnp.log(1.0 - 2.0 ** (-5.0 - jnp.arange(RET_HEADS, dtype=jnp.float32)))
    chunk = min(RET_CHUNK, T)
    sbw = SB_HEADS * SB_HD
    new_states = []
    k_sh = None
    v_sh = None
    for l in range(DEPTH):
        h = rmsnorm(x, g_mix[l])
        if l < N_A_LAYERS:
            o, s = retention_layer(h, pos, ret_s0[l], ret_w_in[l], ret_wo[l], log_gamma, chunk)
            new_states.append(s)
        else:
            if l == N_A_LAYERS:
                kv = rmsnorm(x, g_kv) @ kv_w
                k_sh = kv[..., :sbw].reshape(B, T, SB_HEADS, SB_HD)
                v_sh = kv[..., sbw:].reshape(B, T, SB_HEADS, SB_HD)
            lb = l - N_A_LAYERS
            q = (h @ sb_wq[lb]).reshape(B, T, SB_HEADS, SB_HD)
            o = sb_fn(q, k_sh, v_sh, sb_bias[lb]).reshape(B, T, sbw) @ sb_wo[lb]
        x = x + o
        hm = rmsnorm(x, g_ffn[l]).reshape(B * T, D)
        x = x + hier_moe(hm, moe_wg[l], moe_bg[l], moe_we[l], moe_be[l], moe_w1[l], moe_w3[l], moe_w2[l]).reshape(B, T, D)
    return rmsnorm(x, g_final), jnp.stack(new_states), k_sh, v_sh


def setup_inputs(seed: int = 0) -> dict:
    key = jax.random.key(seed)
    ks = jax.random.split(key, 23)
    f32 = jnp.float32
    n_pages = PAST_LEN // PAGE_SIZE
    n_used = DEC_BATCH * n_pages
    n_pool = n_used + max(1, n_used // 4)
    qk_w = RET_HEADS * RET_DK
    v_w = RET_HEADS * RET_DV
    sbw = SB_HEADS * SB_HD

    def dense(k, shape, fan_in):
        return jax.random.normal(k, shape, f32) * (fan_in ** -0.5)

    def gain(k, shape):
        return 1.0 + 0.02 * jax.random.normal(k, shape, f32)

    page_table = jax.random.permutation(ks[5], n_pool)[:n_used].reshape(DEC_BATCH, n_pages).astype(jnp.int32)
    return {
        'x_prompt': jax.random.normal(ks[0], (BATCH, SEQ, D_MODEL), f32),
        'x_sample': jax.random.normal(ks[1], (DEC_BATCH, DEC_SEQ, D_MODEL), f32),
        'state_ret': 0.5 * jax.random.normal(ks[2], (N_A_LAYERS, DEC_BATCH, RET_HEADS, RET_DK, RET_DV), f32),
        'cache_k': jax.random.normal(ks[3], (n_pool, PAGE_SIZE, SB_HEADS, SB_HD), f32),
        'cache_v': jax.random.normal(ks[4], (n_pool, PAGE_SIZE, SB_HEADS, SB_HD), f32),
        'page_table': page_table,
        'g_mix': gain(ks[6], (DEPTH, D_MODEL)),
        'g_ffn': gain(ks[7], (DEPTH, D_MODEL)),
        'g_kv': gain(ks[8], (D_MODEL,)),
        'g_final': gain(ks[9], (D_MODEL,)),
        'ret_w_in': dense(ks[10], (N_A_LAYERS, D_MODEL, 2 * qk_w + 2 * v_w), D_MODEL),
        'ret_wo': dense(ks[11], (N_A_LAYERS, v_w, D_MODEL), v_w),
        'kv_w': dense(ks[12], (D_MODEL, 2 * sbw), D_MODEL),
        'sb_wq': dense(ks[13], (N_B_LAYERS, D_MODEL, sbw), D_MODEL),
        'sb_wo': dense(ks[14], (N_B_LAYERS, sbw, D_MODEL), sbw),
        'sb_bias': SB_BIAS_INIT + 0.5 * jax.random.normal(ks[22], (N_B_LAYERS, SB_HEADS), f32),
        'moe_wg': dense(ks[15], (DEPTH, D_MODEL, N_GROUPS), D_MODEL),
        'moe_bg': 0.01 * jax.random.normal(ks[16], (DEPTH, N_GROUPS), f32),
        'moe_we': dense(ks[17], (DEPTH, D_MODEL, N_EXPERTS), D_MODEL),
        'moe_be': 0.01 * jax.random.normal(ks[18], (DEPTH, N_EXPERTS), f32),
        'moe_w1': dense(ks[19], (DEPTH, N_EXPERTS, D_MODEL, D_EXPERT), D_MODEL),
        'moe_w3': dense(ks[20], (DEPTH, N_EXPERTS, D_MODEL, D_EXPERT), D_MODEL),
        'moe_w2': dense(ks[21], (DEPTH, N_EXPERTS, D_EXPERT, D_MODEL), D_EXPERT),
    }


def reference(x_prompt, x_sample, state_ret, cache_k, cache_v, page_table, g_mix, g_ffn, g_kv, g_final,
              ret_w_in, ret_wo, kv_w, sb_wq, sb_wo, sb_bias, moe_wg, moe_bg, moe_we, moe_be, moe_w1, moe_w3, moe_w2):
    past_len = page_table.shape[1] * PAGE_SIZE
    pos_p = jnp.arange(x_prompt.shape[1])
    pos_s = past_len + jnp.arange(x_sample.shape[1])
    s0_p = jnp.zeros((N_A_LAYERS, x_prompt.shape[0]) + state_ret.shape[2:], state_ret.dtype)
    sb_s = functools.partial(sb_sample, k_cache=cache_k, v_cache=cache_v, page_table=page_table)
    y_p, st_p, k_p, v_p = trunk(x_prompt, pos_p, s0_p, sb_prompt, g_mix, g_ffn, g_kv, g_final,
                                ret_w_in, ret_wo, kv_w, sb_wq, sb_wo, sb_bias,
                                moe_wg, moe_bg, moe_we, moe_be, moe_w1, moe_w3, moe_w2)
    y_s, st_s, k_s, v_s = trunk(x_sample, pos_s, state_ret, sb_s, g_mix, g_ffn, g_kv, g_final,
                                ret_w_in, ret_wo, kv_w, sb_wq, sb_wo, sb_bias,
                                moe_wg, moe_bg, moe_we, moe_be, moe_w1, moe_w3, moe_w2)
    return (y_p, y_s, st_p, k_p, v_p, st_s, k_s, v_s)
```

```python
import functools

import numpy as np
import jax
import jax.numpy as jnp
from jax import lax
from jax.experimental import pallas as pl
from jax.experimental.pallas import tpu as pltpu

F32 = jnp.float32
BF16 = jnp.bfloat16
I32 = jnp.int32

D_MODEL = 1024
RET_HEADS = 4
RET_DK = 256
RET_DV = 512
RET_CHUNK = 128
ROPE_BASE = 10000.0
SB_HEADS = 8
SB_HD = 128
SB_SCALE = SB_HD ** -0.5
PAGE_SIZE = 128
N_GROUPS = 4
EXPERTS_PER_GROUP = 8
N_EXPERTS = N_GROUPS * EXPERTS_PER_GROUP
TOP_K = 2
D_EXPERT = 512
EPS = 1e-6

LANES = 128
SUBLANES = 8
VMEM_LIMIT = 56 * 1024 * 1024
SAMPLE_T_PAD = SUBLANES
ROUTE_GROUP_LANE = N_EXPERTS

_NT = (((1,), (1,)), ((), ()))
_NN = (((1,), (0,)), ((), ()))


def _mm(a, b, n_pass, dims=_NN):
    dot = lambda x, y: lax.dot_general(x, y, dims, preferred_element_type=F32)
    a_hi = a.astype(BF16)
    b_hi = b.astype(BF16)
    out = dot(a_hi, b_hi)
    if n_pass == 1:
        return out
    a_lo = (a - a_hi.astype(F32)).astype(BF16)
    b_lo = (b - b_hi.astype(F32)).astype(BF16)
    return out + dot(a_hi, b_lo) + dot(a_lo, b_hi)


def _rms_scale(x):
    return x * lax.rsqrt(jnp.mean(x * x, axis=-1, keepdims=True) + EPS)


def _params(sem):
    return pltpu.CompilerParams(dimension_semantics=sem, vmem_limit_bytes=VMEM_LIMIT)


def _norm_proj_kernel(*refs, has_moe, out_ranges, n_pass):
    it = iter(refs)
    x_ref = next(it)
    moe_ref = next(it) if has_moe else None
    route_ref = next(it) if has_moe else None
    g_ref = next(it)
    w_ref = next(it)
    xnew_ref = next(it) if has_moe else None
    out_refs = [next(it) for _ in out_ranges]
    xhat_ref = next(it)

    j = pl.program_id(1)

    @pl.when(j == 0)
    def _():
        x = x_ref[...]
        if has_moe:
            d = x.shape[-1]
            r = route_ref[...]
            x = x + (moe_ref[:, :d] * r[:, 2:3] + moe_ref[:, d:] * r[:, 3:4])
            xnew_ref[...] = x
        xhat_ref[...] = _rms_scale(x)

    y = _mm(xhat_ref[...] * g_ref[0], w_ref[...], n_pass)
    for o_ref, (lo, hi) in zip(out_refs, out_ranges):
        @pl.when((j >= lo) & (j < hi))
        def _():
            o_ref[...] = y


def _norm_proj(x, gains, weights, *, moe=None, route=None, n_pass, tm, tn=1024):
    n, d = x.shape
    has_moe = moe is not None
    widths = [w.shape[1] for w in weights]
    wdt = BF16 if n_pass == 1 else F32
    wcat = jnp.concatenate([w.astype(wdt) for w in weights], axis=1)
    gcat = jnp.concatenate(
        [jnp.broadcast_to(g[None, None, :], (w // tn, 1, d)) for g, w in zip(gains, widths)], axis=0)
    out_ranges, lo = [], 0
    for w in widths:
        out_ranges.append((lo, lo + w // tn))
        lo += w // tn
    n_chunks = lo

    in_specs = [pl.BlockSpec((tm, d), lambda i, j: (i, 0))]
    args = [x]
    if has_moe:
        in_specs += [pl.BlockSpec((tm, 2 * d), lambda i, j: (i, 0)),
                     pl.BlockSpec((tm, LANES), lambda i, j: (i, 0))]
        args += [moe, route]
    in_specs += [pl.BlockSpec((1, 1, d), lambda i, j: (j, 0, 0)),
                 pl.BlockSpec((d, tn), lambda i, j: (0, j))]
    args += [gcat, wcat]

    out_shape, out_specs = [], []
    if has_moe:
        out_shape.append(jax.ShapeDtypeStruct((n, d), F32))
        out_specs.append(pl.BlockSpec((tm, d), lambda i, j: (i, 0)))
    for w, (a, b) in zip(widths, out_ranges):
        out_shape.append(jax.ShapeDtypeStruct((n, w), F32))
        out_specs.append(pl.BlockSpec(
            (tm, tn), lambda i, j, a=a, b=b: (i, jnp.clip(j - a, 0, b - a - 1))))

    return pl.pallas_call(
        functools.partial(_norm_proj_kernel, has_moe=has_moe, out_ranges=tuple(out_ranges), n_pass=n_pass),
        grid=(n // tm, n_chunks),
        in_specs=in_specs,
        out_specs=out_specs,
        out_shape=out_shape,
        scratch_shapes=[pltpu.VMEM((tm, d), F32)],
        compiler_params=_params(("parallel", "arbitrary")),
        name="norm_proj",
    )(*args)


def _final_norm_kernel(x_ref, moe_ref, route_ref, g_ref, y_ref):
    d = x_ref.shape[-1]
    r = route_ref[...]
    x = x_ref[...] + (moe_ref[:, :d] * r[:, 2:3] + moe_ref[:, d:] * r[:, 3:4])
    y_ref[...] = _rms_scale(x) * g_ref[...]


def _final_norm(x, moe, route, g, *, tm):
    n, d = x.shape
    return pl.pallas_call(
        _final_norm_kernel,
        grid=(n // tm,),
        in_specs=[pl.BlockSpec((tm, d), lambda i: (i, 0)),
                  pl.BlockSpec((tm, 2 * d), lambda i: (i, 0)),
                  pl.BlockSpec((tm, LANES), lambda i: (i, 0)),
                  pl.BlockSpec((1, d), lambda i: (0, 0))],
        out_specs=pl.BlockSpec((tm, d), lambda i: (i, 0)),
        out_shape=jax.ShapeDtypeStruct((n, d), F32),
        compiler_params=_params(("parallel",)),
        name="final_norm",
    )(x, moe, route, g[None, :])


def _retention_kernel(lg_ref, q_ref, k_ref, v_ref, cos_ref, sin_ref, s0_ref, o_ref, s_ref,
                      *, chunk_true, n_pass, zero_init):
    h = pl.program_id(1)
    c = pl.program_id(2)
    cb = q_ref.shape[1]
    half = RET_DK // 2

    @pl.when(c == 0)
    def _():
        if zero_init:
            s_ref[0, 0] = jnp.zeros(s_ref.shape[2:], F32)
        else:
            s_ref[0, 0] = s0_ref[0, 0]

    lg = lg_ref[h]
    cos = cos_ref[...]
    sin = sin_ref[...]

    def rot(x):
        x1, x2 = x[:, :half], x[:, half:]
        return jnp.concatenate([x1 * cos - x2 * sin, x1 * sin + x2 * cos], axis=-1)

    q = rot(q_ref[0])
    k = rot(k_ref[0]) * (RET_DK ** -0.5)
    v = v_ref[0]

    ii = lax.broadcasted_iota(I32, (cb, cb), 0)
    jj = lax.broadcasted_iota(I32, (cb, cb), 1)
    causal = ii >= jj
    dmask = jnp.where(causal, jnp.exp(jnp.where(causal, (ii - jj).astype(F32), 0.0) * lg), 0.0)
    ri = lax.broadcasted_iota(I32, (cb, 1), 0).astype(F32)
    q_dec = jnp.exp((ri + 1.0) * lg)
    k_dec = jnp.exp((chunk_true - 1.0 - ri) * lg)
    c_dec = jnp.exp(jnp.full((1, RET_DV), float(chunk_true), F32) * lg)

    s = s_ref[0, 0]
    att = _mm(q, k, n_pass, _NT) * dmask
    o = _mm(att, v, n_pass) + _mm(q, s, n_pass) * q_dec
    kd = k * k_dec
    if cb < LANES:
        kd = jnp.concatenate([kd, jnp.zeros((LANES - cb, RET_DK), F32)], axis=0)
        v = jnp.concatenate([v, jnp.zeros((LANES - cb, RET_DV), F32)], axis=0)
    s_ref[0, 0] = s * c_dec + _mm(kd.T, v, n_pass)
    o_ref[0] = _rms_scale(o)


def _retention(proj, cos, sin, s0, log_gamma, *, batch, seq, chunk_blk, chunk_true, n_pass):
    nc = seq // chunk_blk
    proj3 = proj.reshape(batch, seq, proj.shape[-1])
    kcol = RET_HEADS
    vcol = 2 * RET_HEADS * RET_DK // RET_DV
    zero_init = s0 is None
    if zero_init:
        s0 = jnp.zeros((1, 1, RET_DK, RET_DV), F32)
        s0_map = lambda b, h, c, lg: (0, 0, 0, 0)
    else:
        s0_map = lambda b, h, c, lg: (b, h, 0, 0)
    grid_spec = pltpu.PrefetchScalarGridSpec(
        num_scalar_prefetch=1,
        grid=(batch, RET_HEADS, nc),
        in_specs=[
            pl.BlockSpec((1, chunk_blk, RET_DK), lambda b, h, c, lg: (b, c, h)),
            pl.BlockSpec((1, chunk_blk, RET_DK), lambda b, h, c, lg: (b, c, kcol + h)),
            pl.BlockSpec((1, chunk_blk, RET_DV), lambda b, h, c, lg: (b, c, vcol + h)),
            pl.BlockSpec((chunk_blk, RET_DK // 2), lambda b, h, c, lg: (c, 0)),
            pl.BlockSpec((chunk_blk, RET_DK // 2), lambda b, h, c, lg: (c, 0)),
            pl.BlockSpec((1, 1, RET_DK, RET_DV), s0_map),
        ],
        out_specs=[
            pl.BlockSpec((1, chunk_blk, RET_DV), lambda b, h, c, lg: (b, c, h)),
            pl.BlockSpec((1, 1, RET_DK, RET_DV), lambda b, h, c, lg: (b, h, 0, 0)),
        ],
    )
    o, s = pl.pallas_call(
        functools.partial(_retention_kernel, chunk_true=chunk_true, n_pass=n_pass, zero_init=zero_init),
        grid_spec=grid_spec,
        out_shape=[jax.ShapeDtypeStruct((batch, seq, RET_HEADS * RET_DV), F32),
                   jax.ShapeDtypeStruct((batch, RET_HEADS, RET_DK, RET_DV), F32)],
        compiler_params=_params(("parallel", "parallel", "arbitrary")),
        name="retention",
    )(log_gamma, proj3, proj3, proj3, cos, sin, s0)
    return o.reshape(batch * seq, RET_HEADS * RET_DV), s


def _route(logits):
    lane = lax.broadcasted_iota(I32, logits.shape, 1)
    neg = -jnp.inf
    big = 4 * LANES
    gmask = (lane >= ROUTE_GROUP_LANE) & (lane < ROUTE_GROUP_LANE + N_GROUPS)
    gl = jnp.where(gmask, logits, neg)
    gmax = jnp.max(gl, axis=-1, keepdims=True)
    gsel = jnp.min(jnp.where(gl == gmax, lane, big), axis=-1, keepdims=True) - ROUTE_GROUP_LANE
    g_w = 1.0 / jnp.sum(jnp.where(gmask, jnp.exp(gl - gmax), 0.0), axis=-1, keepdims=True)
    e_lo = gsel * EXPERTS_PER_GROUP
    emask = (lane >= e_lo) & (lane < e_lo + EXPERTS_PER_GROUP)
    el = jnp.where(emask, logits, neg)
    v1 = jnp.max(el, axis=-1, keepdims=True)
    i1 = jnp.min(jnp.where(el == v1, lane, big), axis=-1, keepdims=True)
    el2 = jnp.where(lane == i1, neg, el)
    v2 = jnp.max(el2, axis=-1, keepdims=True)
    i2 = jnp.min(jnp.where(el2 == v2, lane, big), axis=-1, keepdims=True)
    e2 = jnp.exp(v2 - v1)
    inv = 1.0 / (1.0 + e2)
    gate1 = g_w * inv
    gate2 = g_w * (e2 * inv)
    out = jnp.where(lane == 0, i1.astype(F32), 0.0)
    out = jnp.where(lane == 1, i2.astype(F32), out)
    out = jnp.where(lane == 2, gate1, out)
    out = jnp.where(lane == 3, gate2, out)
    return out


def _out_router_kernel(*refs, has_gate, n_pass):
    it = iter(refs)
    a_ref = next(it)
    gate_ref = next(it) if has_gate else None
    w_ref, x_ref, gf_ref, wr_ref, br_ref, x1_ref, hm_ref, route_ref = it
    a = a_ref[...]
    if has_gate:
        g = gate_ref[...]
        a = a * (g * (1.0 / (1.0 + jnp.exp(-g))))
    x1 = x_ref[...] + _mm(a, w_ref[...], n_pass)
    x1_ref[...] = x1
    hm = _rms_scale(x1) * gf_ref[...]
    hm_ref[...] = hm
    logits = _mm(hm, wr_ref[...], 3) + br_ref[...]
    route_ref[...] = _route(logits)


def _out_router(a, gate_src, w, x, g_ffn, wg, bg, we, be, *, n_pass, tm):
    n, d = x.shape
    ka = a.shape[1]
    has_gate = gate_src is not None
    wdt = BF16 if n_pass == 1 else F32
    wr = jnp.zeros((d, LANES), F32).at[:, :N_EXPERTS].set(we).at[:, N_EXPERTS:N_EXPERTS + N_GROUPS].set(wg)
    br = jnp.zeros((1, LANES), F32).at[0, :N_EXPERTS].set(be).at[0, N_EXPERTS:N_EXPERTS + N_GROUPS].set(bg)
    in_specs = [pl.BlockSpec((tm, ka), lambda i: (i, 0))]
    args = [a]
    if has_gate:
        gblk = gate_src.shape[1] // ka - 1
        in_specs.append(pl.BlockSpec((tm, ka), lambda i: (i, gblk)))
        args.append(gate_src)
    in_specs += [pl.BlockSpec((ka, d), lambda i: (0, 0)),
                 pl.BlockSpec((tm, d), lambda i: (i, 0)),
                 pl.BlockSpec((1, d), lambda i: (0, 0)),
                 pl.BlockSpec((d, LANES), lambda i: (0, 0)),
                 pl.BlockSpec((1, LANES), lambda i: (0, 0))]
    args += [w.astype(wdt), x, g_ffn[None, :], wr, br]
    return pl.pallas_call(
        functools.partial(_out_router_kernel, has_gate=has_gate, n_pass=n_pass),
        grid=(n // tm,),
        in_specs=in_specs,
        out_specs=[pl.BlockSpec((tm, d), lambda i: (i, 0)),
                   pl.BlockSpec((tm, d), lambda i: (i, 0)),
                   pl.BlockSpec((tm, LANES), lambda i: (i, 0))],
        out_shape=[jax.ShapeDtypeStruct((n, d), F32),
                   jax.ShapeDtypeStruct((n, d), F32),
                   jax.ShapeDtypeStruct((n, LANES), F32)],
        compiler_params=_params(("parallel",)),
        name="out_router",
    )(*args)


def _moe_kernel(tab_ref, bexp_ref, nused_ref, hm_hbm, w1_ref, w3_ref, w2_ref, out_hbm,
                xbuf, obuf, gsem, ssem, *cast_refs, bm, n_pass):
    i = pl.program_id(0)
    nb_used = nused_ref[0]
    slot = i % 2

    def gather_copy(blk, sl, r):
        tkv = tab_ref[blk * bm + r]
        row = jnp.where(tkv >= 0, tkv >> 1, 0)
        return pltpu.make_async_copy(hm_hbm.at[pl.ds(row, 1)], xbuf.at[sl, pl.ds(r, 1)], gsem.at[sl])

    def scatter_copy(blk, sl, r):
        tkv = tab_ref[blk * bm + r]
        return pltpu.make_async_copy(obuf.at[sl, pl.ds(r, 1)], out_hbm.at[pl.ds(tkv, 1)], ssem.at[sl])

    def for_rows(make, blk, sl, start, skip_padding=False):
        def body(r, carry):
            def issue():
                cp = make(blk, sl, r)
                if start:
                    cp.start()
                else:
                    cp.wait()
            if skip_padding:
                pl.when(tab_ref[blk * bm + r] >= 0)(issue)
            else:
                issue()
            return carry
        lax.fori_loop(0, bm, body, 0)

    scatter_rows = functools.partial(for_rows, scatter_copy, skip_padding=True)

    @pl.when(i < nb_used)
    def _():
        @pl.when(i == 0)
        def _():
            for_rows(gather_copy, 0, 0, True)

        @pl.when(i + 1 < nb_used)
        def _():
            for_rows(gather_copy, i + 1, 1 - slot, True)

        for_rows(gather_copy, i, slot, False)
        x = xbuf[slot]

        if n_pass == 1:
            w1c, w3c, w2c = cast_refs
            changed = jnp.logical_or(i == 0, bexp_ref[i] != bexp_ref[jnp.maximum(i - 1, 0)])

            @pl.when(changed)
            def _():
                w1c[...] = w1_ref[0].astype(BF16)
                w3c[...] = w3_ref[0].astype(BF16)
                w2c[...] = w2_ref[0].astype(BF16)

            xb = x.astype(BF16)
            h1 = jnp.dot(xb, w1c[...], preferred_element_type=F32)
            h3 = jnp.dot(xb, w3c[...], preferred_element_type=F32)
            hmid = h1 * (1.0 / (1.0 + jnp.exp(-h1))) * h3
            y = jnp.dot(hmid.astype(BF16), w2c[...], preferred_element_type=F32)
        else:
            h1 = _mm(x, w1_ref[0], n_pass)
            h3 = _mm(x, w3_ref[0], n_pass)
            hmid = h1 * (1.0 / (1.0 + jnp.exp(-h1))) * h3
            y = _mm(hmid, w2_ref[0], n_pass)

        @pl.when(i >= 2)
        def _():
            scatter_rows(i - 2, slot, False)

        obuf[slot] = y
        scatter_rows(i, slot, True)

        @pl.when(i == nb_used - 1)
        def _():
            scatter_rows(i, slot, False)

            @pl.when(i >= 1)
            def _():
                scatter_rows(i - 1, 1 - slot, False)


def _dispatch_tables(eid, bm):
    n_assign = eid.size
    e_flat = eid.reshape(-1)
    order = jnp.argsort(e_flat).astype(I32)
    e_s = e_flat[order]
    counts = jnp.bincount(e_flat, length=N_EXPERTS).astype(I32)
    start = jnp.cumsum(counts) - counts
    padded = ((counts + bm - 1) // bm) * bm
    pend = jnp.cumsum(padded)
    pstart = pend - padded
    dest = pstart[e_s] + (jnp.arange(n_assign, dtype=I32) - start[e_s])
    nb = n_assign // bm + N_EXPERTS
    tab = jnp.full((nb * bm,), -1, I32).at[dest].set(order)
    nb_used = pend[-1] // bm
    blk = jnp.minimum(jnp.arange(nb, dtype=I32), nb_used - 1)
    bexp = jnp.minimum(jnp.searchsorted(pend, blk * bm, side="right"), N_EXPERTS - 1).astype(I32)
    return tab, bexp, nb_used.reshape(1).astype(I32)


def _moe(hm, route, w1, w3, w2, *, n_pass, bm):
    n, d = hm.shape
    eid = route[:, :TOP_K].astype(I32)
    tab, bexp, nused = _dispatch_tables(eid, bm)
    nb = bexp.shape[0]
    scratch = [pltpu.VMEM((2, bm, d), F32), pltpu.VMEM((2, bm, d), F32),
               pltpu.SemaphoreType.DMA((2,)), pltpu.SemaphoreType.DMA((2,))]
    if n_pass == 1:
        scratch += [pltpu.VMEM((d, D_EXPERT), BF16), pltpu.VMEM((d, D_EXPERT), BF16),
                    pltpu.VMEM((D_EXPERT, d), BF16)]
    grid_spec = pltpu.PrefetchScalarGridSpec(
        num_scalar_prefetch=3,
        grid=(nb,),
        in_specs=[
            pl.BlockSpec(memory_space=pl.ANY),
            pl.BlockSpec((1, d, D_EXPERT), lambda i, tab, bexp, nu: (bexp[i], 0, 0)),
            pl.BlockSpec((1, d, D_EXPERT), lambda i, tab, bexp, nu: (bexp[i], 0, 0)),
            pl.BlockSpec((1, D_EXPERT, d), lambda i, tab, bexp, nu: (bexp[i], 0, 0)),
        ],
        out_specs=pl.BlockSpec(memory_space=pl.ANY),
        scratch_shapes=scratch,
    )
    out = pl.pallas_call(
        functools.partial(_moe_kernel, bm=bm, n_pass=n_pass),
        grid_spec=grid_spec,
        out_shape=jax.ShapeDtypeStruct((TOP_K * n, d), F32),
        compiler_params=_params(("arbitrary",)),
        name="moe_experts",
    )(tab, bexp, nused, hm, w1, w3, w2)
    return out.reshape(n, TOP_K * d)


def _sb_block(z, valid, u, cacc):
    sp = jnp.maximum(z, 0.0) + jnp.log1p(jnp.exp(-jnp.abs(z)))
    if valid is not None:
        sp = jnp.where(valid, sp, 0.0)
    sp_hi = sp.astype(BF16)
    sp_lo = (sp - sp_hi.astype(F32)).astype(BF16)
    later = (jnp.dot(sp_hi, u, preferred_element_type=F32)
             + jnp.dot(sp_lo, u, preferred_element_type=F32))
    a = jnp.exp((z - sp) - later - cacc)
    if valid is not None:
        a = jnp.where(valid, a, 0.0)
    return a, cacc + jnp.sum(sp, axis=-1, keepdims=True)


def _strict_lower(n):
    return jnp.asarray(np.tril(np.ones((n, n), np.float32), -1), BF16)


def _sb_prompt_kernel(bias_ref, q_ref, k_ref, v_ref, u_ref, o_ref, *, tq):
    h = pl.program_id(1)
    i = pl.program_id(2)
    bias = bias_ref[h]
    qs = (q_ref[...] * SB_SCALE).astype(BF16)
    u = u_ref[...]

    def block(kstart, masked, acc, cacc):
        kb = k_ref[pl.ds(kstart, tq), :].astype(BF16)
        vb = v_ref[pl.ds(kstart, tq), :].astype(BF16)
        z = lax.dot_general(qs, kb, _NT, preferred_element_type=F32) + bias
        valid = None
        if masked:
            row = lax.broadcasted_iota(I32, (tq, tq), 0)
            col = lax.broadcasted_iota(I32, (tq, tq), 1)
            valid = col < row
        a, cacc = _sb_block(z, valid, u, cacc)
        return acc + jnp.dot(a.astype(BF16), vb, preferred_element_type=F32), cacc

    acc = jnp.zeros((tq, SB_HD), F32)
    cacc = jnp.zeros((tq, 1), F32)
    acc, cacc = block(pl.multiple_of(i * tq, tq), True, acc, cacc)

    def body(j, carry):
        return block(pl.multiple_of((i - 1 - j) * tq, tq), False, *carry)

    acc, cacc = lax.fori_loop(0, i, body, (acc, cacc))
    o_ref[...] = acc


def _sb_prompt(q, k, v, bias, *, batch, seq, tq=256):
    n = batch * seq
    nq = seq // tq
    grid_spec = pltpu.PrefetchScalarGridSpec(
        num_scalar_prefetch=1,
        grid=(batch, SB_HEADS, nq),
        in_specs=[
            pl.BlockSpec((tq, SB_HD), lambda b, h, i, bias: (b * nq + i, h)),
            pl.BlockSpec((seq, SB_HD), lambda b, h, i, bias: (b, h)),
            pl.BlockSpec((seq, SB_HD), lambda b, h, i, bias: (b, h)),
            pl.BlockSpec((tq, tq), lambda b, h, i, bias: (0, 0)),
        ],
        out_specs=pl.BlockSpec((tq, SB_HD), lambda b, h, i, bias: (b * nq + i, h)),
    )
    return pl.pallas_call(
        functools.partial(_sb_prompt_kernel, tq=tq),
        grid_spec=grid_spec,
        out_shape=jax.ShapeDtypeStruct((n, SB_HEADS * SB_HD), F32),
        compiler_params=_params(("parallel", "parallel", "arbitrary")),
        name="sb_prompt",
    )(bias, q, k, v, _strict_lower(tq))


def _sb_sample_kernel(pt_ref, bias_ref, q_ref, kn_ref, vn_ref, u_ref, *rest, n_group):
    page_refs = rest[:2 * n_group]
    o_ref, qh_ref, acc_ref, cacc_ref = rest[2 * n_group:]
    g = pl.program_id(1)
    tp = q_ref.shape[0]
    m = SB_HEADS * tp
    u = u_ref[...]
    row = lax.broadcasted_iota(I32, (m, 1), 0)
    bias_col = jnp.zeros((m, 1), F32)
    for h in range(SB_HEADS):
        bias_col = jnp.where(row // tp == h, bias_ref[h], bias_col)

    def sweep(kp, vp, valid):
        qh = qh_ref[...]
        z = jnp.concatenate(
            [lax.dot_general(qh[h * tp:(h + 1) * tp], kp[:, h * SB_HD:(h + 1) * SB_HD].astype(BF16), _NT,
                             preferred_element_type=F32) for h in range(SB_HEADS)], axis=0) + bias_col
        a, cacc = _sb_block(z, valid, u, cacc_ref[...])
        cacc_ref[...] = cacc
        ab = a.astype(BF16)
        acc_ref[...] += jnp.concatenate(
            [jnp.dot(ab[h * tp:(h + 1) * tp], vp[:, h * SB_HD:(h + 1) * SB_HD].astype(BF16),
                     preferred_element_type=F32) for h in range(SB_HEADS)], axis=0)

    @pl.when(g == 0)
    def _():
        q = q_ref[...] * SB_SCALE
        qh_ref[...] = jnp.concatenate(
            [q[:, h * SB_HD:(h + 1) * SB_HD] for h in range(SB_HEADS)], axis=0).astype(BF16)
        acc_ref[...] = jnp.zeros(acc_ref.shape, F32)
        cacc_ref[...] = jnp.zeros(cacc_ref.shape, F32)
        pad = jnp.zeros((PAGE_SIZE - tp, SB_HEADS * SB_HD), F32)
        col = lax.broadcasted_iota(I32, (m, PAGE_SIZE), 1)
        qpos = lax.broadcasted_iota(I32, (m, PAGE_SIZE), 0) % tp
        sweep(jnp.concatenate([kn_ref[...], pad], axis=0),
              jnp.concatenate([vn_ref[...], pad], axis=0), col < qpos)

    for jj in range(n_group):
        sweep(page_refs[2 * jj][0], page_refs[2 * jj + 1][0], None)

    @pl.when(g == pl.num_programs(1) - 1)
    def _():
        acc = acc_ref[...]
        o_ref[...] = jnp.concatenate([acc[h * tp:(h + 1) * tp] for h in range(SB_HEADS)], axis=1)


def _sb_sample(q, k_new, v_new, bias, cache_k, cache_v, page_table, *, n_group=8):
    batch, n_pages = page_table.shape
    tp = q.shape[0] // batch
    n_pool = cache_k.shape[0]
    width = SB_HEADS * SB_HD
    ck = cache_k.reshape(n_pool, PAGE_SIZE, width)
    cv = cache_v.reshape(n_pool, PAGE_SIZE, width)
    m = SB_HEADS * tp

    def page_spec(jj):
        return pl.BlockSpec(
            (1, PAGE_SIZE, width),
            lambda b, g, pt, bias, jj=jj: (pt[b, n_pages - 1 - (g * n_group + jj)], 0, 0))

    row_spec = pl.BlockSpec((tp, width), lambda b, g, pt, bias: (b, 0))
    in_specs = [row_spec, row_spec, row_spec,
                pl.BlockSpec((PAGE_SIZE, PAGE_SIZE), lambda b, g, pt, bias: (0, 0))]
    args = [q, k_new, v_new, _strict_lower(PAGE_SIZE)]
    for jj in range(n_group):
        in_specs += [page_spec(jj), page_spec(jj)]
        args += [ck, cv]
    grid_spec = pltpu.PrefetchScalarGridSpec(
        num_scalar_prefetch=2,
        grid=(batch, n_pages // n_group),
        in_specs=in_specs,
        out_specs=row_spec,
        scratch_shapes=[pltpu.VMEM((m, SB_HD), BF16), pltpu.VMEM((m, SB_HD), F32), pltpu.VMEM((m, 1), F32)],
    )
    return pl.pallas_call(
        functools.partial(_sb_sample_kernel, n_group=n_group),
        grid_spec=grid_spec,
        out_shape=jax.ShapeDtypeStruct(q.shape, F32),
        compiler_params=_params(("parallel", "arbitrary")),
        name="sb_sample",
    )(page_table, bias, *args)


def _rope_tables(pos):
    half = RET_DK // 2
    inv = ROPE_BASE ** (-jnp.arange(half, dtype=F32) / half)
    ang = pos.astype(F32)[:, None] * inv[None, :]
    return jnp.cos(ang), jnp.sin(ang)


def _trunk(x, pos, s0, attend, p, *, batch, seq, chunk_blk, chunk_true, n_pass, tm, bm):
    log_gamma = jnp.log(1.0 - 2.0 ** (-5.0 - jnp.arange(RET_HEADS, dtype=F32)))
    cos, sin = _rope_tables(pos)

    (proj,) = _norm_proj(x, [p["g_mix"][0]], [p["ret_w_in"][0]], n_pass=n_pass, tm=tm)
    o, state = _retention(proj, cos, sin, s0, log_gamma, batch=batch, seq=seq,
                          chunk_blk=chunk_blk, chunk_true=chunk_true, n_pass=n_pass)
    x1, hm, route = _out_router(o, proj, p["ret_wo"][0], x, p["g_ffn"][0], p["moe_wg"][0], p["moe_bg"][0],
                                p["moe_we"][0], p["moe_be"][0], n_pass=n_pass, tm=tm)
    moe = _moe(hm, route, p["moe_w1"][0], p["moe_w3"][0], p["moe_w2"][0], n_pass=n_pass, bm=bm)

    sbw = SB_HEADS * SB_HD
    x2, k_sh, v_sh, q = _norm_proj(
        x1, [p["g_kv"], p["g_kv"], p["g_mix"][1]],
        [p["kv_w"][:, :sbw], p["kv_w"][:, sbw:], p["sb_wq"][0]],
        moe=moe, route=route, n_pass=n_pass, tm=tm)
    att = attend(q, k_sh, v_sh, p["sb_bias"][0])
    x3, hm, route = _out_router(att, None, p["sb_wo"][0], x2, p["g_ffn"][1], p["moe_wg"][1], p["moe_bg"][1],
                                p["moe_we"][1], p["moe_be"][1], n_pass=n_pass, tm=tm)
    moe = _moe(hm, route, p["moe_w1"][1], p["moe_w3"][1], p["moe_w2"][1], n_pass=n_pass, bm=bm)
    y = _final_norm(x3, moe, route, p["g_final"], tm=tm)
    return y, state, k_sh, v_sh


def kernel(x_prompt, x_sample, state_ret, cache_k, cache_v, page_table, g_mix, g_ffn, g_kv, g_final,
           ret_w_in, ret_wo, kv_w, sb_wq, sb_wo, sb_bias, moe_wg, moe_bg, moe_we, moe_be, moe_w1, moe_w3, moe_w2):
    p = dict(g_mix=g_mix, g_ffn=g_ffn, g_kv=g_kv, g_final=g_final, ret_w_in=ret_w_in, ret_wo=ret_wo,
             kv_w=kv_w, sb_wq=sb_wq, sb_wo=sb_wo, sb_bias=sb_bias, moe_wg=moe_wg, moe_bg=moe_bg,
             moe_we=moe_we, moe_be=moe_be, moe_w1=moe_w1, moe_w3=moe_w3, moe_w2=moe_w2)
    bp, tp, d = x_prompt.shape
    bs, ts, _ = x_sample.shape
    past_len = page_table.shape[1] * PAGE_SIZE

    y_p, st_p, k_p, v_p = _trunk(
        x_prompt.reshape(bp * tp, d), jnp.arange(tp), None,
        functools.partial(_sb_prompt, batch=bp, seq=tp), p,
        batch=bp, seq=tp, chunk_blk=RET_CHUNK, chunk_true=RET_CHUNK, n_pass=1, tm=512, bm=256)

    tpad = SAMPLE_T_PAD
    xs = jnp.pad(x_sample, ((0, 0), (0, tpad - ts), (0, 0))).reshape(bs * tpad, d)
    attend_s = functools.partial(_sb_sample, cache_k=cache_k, cache_v=cache_v, page_table=page_table)
    y_s, st_s, k_s, v_s = _trunk(
        xs, past_len + jnp.arange(tpad), state_ret[0],
        lambda q, k, v, bias: attend_s(q, k, v, bias), p,
        batch=bs, seq=tpad, chunk_blk=tpad, chunk_true=ts, n_pass=3, tm=bs * tpad, bm=16)

    def rows(a, b, t):
        return a.reshape(b, t, SB_HEADS, SB_HD)

    def unpad(a):
        return a.reshape(bs, tpad, -1)[:, :ts]

    return (y_p.reshape(bp, tp, d), unpad(y_s), st_p[None],
            rows(k_p, bp, tp), rows(v_p, bp, tp), st_s[None],
            rows(unpad(k_s), bs, ts), rows(unpad(v_s), bs, ts))
```

```python
import functools

import numpy as np
import jax
import jax.numpy as jnp
from jax import lax
from jax.experimental import pallas as pl
from jax.experimental.pallas import tpu as pltpu

F32 = jnp.float32
BF16 = jnp.bfloat16
I32 = jnp.int32

D_MODEL = 1024
RET_HEADS = 4
RET_DK = 256
RET_DV = 512
RET_CHUNK = 128
ROPE_BASE = 10000.0
SB_HEADS = 8
SB_HD = 128
SB_SCALE = SB_HD ** -0.5
PAGE_SIZE = 128
N_GROUPS = 4
EXPERTS_PER_GROUP = 8
N_EXPERTS = N_GROUPS * EXPERTS_PER_GROUP
TOP_K = 2
D_EXPERT = 512
EPS = 1e-6

LANES = 128
SUBLANES = 8
VMEM_LIMIT = 56 * 1024 * 1024
SAMPLE_T_PAD = SUBLANES
ROUTE_GROUP_LANE = N_EXPERTS

_NT = (((1,), (1,)), ((), ()))
_NN = (((1,), (0,)), ((), ()))


def _mm(a, b, dims=_NN):
    return lax.dot_general(a.astype(BF16), b.astype(BF16), dims, preferred_element_type=F32)


def _rms_scale(x):
    return x * lax.rsqrt(jnp.mean(x * x, axis=-1, keepdims=True) + EPS)


def _params(sem):
    return pltpu.CompilerParams(dimension_semantics=sem, vmem_limit_bytes=VMEM_LIMIT)


def _norm_proj_kernel(*refs, has_moe, out_ranges):
    it = iter(refs)
    x_ref = next(it)
    moe_ref = next(it) if has_moe else None
    route_ref = next(it) if has_moe else None
    g_ref = next(it)
    w_ref = next(it)
    xnew_ref = next(it) if has_moe else None
    out_refs = [next(it) for _ in out_ranges]
    xhat_ref = next(it)

    j = pl.program_id(1)

    @pl.when(j == 0)
    def _():
        x = x_ref[...]
        if has_moe:
            r = route_ref[...]
            x = x + (moe_ref[0] * r[:, 2:3] + moe_ref[1] * r[:, 3:4])
            xnew_ref[...] = x
        xhat_ref[...] = _rms_scale(x)

    y = _mm(xhat_ref[...] * g_ref[0], w_ref[...])
    for o_ref, (lo, hi) in zip(out_refs, out_ranges):
        @pl.when((j >= lo) & (j < hi))
        def _():
            o_ref[...] = y


def _norm_proj(x, gains, weights, *, moe=None, route=None, tm, tn=1024):
    n, d = x.shape
    has_moe = moe is not None
    widths = [w.shape[1] for w in weights]
    wcat = jnp.concatenate([w.astype(BF16) for w in weights], axis=1)
    gcat = jnp.concatenate(
        [jnp.broadcast_to(g[None, None, :], (w // tn, 1, d)) for g, w in zip(gains, widths)], axis=0)
    out_ranges, lo = [], 0
    for w in widths:
        out_ranges.append((lo, lo + w // tn))
        lo += w // tn
    n_chunks = lo

    in_specs = [pl.BlockSpec((tm, d), lambda i, j: (i, 0))]
    args = [x]
    if has_moe:
        in_specs += [pl.BlockSpec((TOP_K, tm, d), lambda i, j: (0, i, 0)),
                     pl.BlockSpec((tm, LANES), lambda i, j: (i, 0))]
        args += [moe, route]
    in_specs += [pl.BlockSpec((1, 1, d), lambda i, j: (j, 0, 0)),
                 pl.BlockSpec((d, tn), lambda i, j: (0, j))]
    args += [gcat, wcat]

    out_shape, out_specs = [], []
    if has_moe:
        out_shape.append(jax.ShapeDtypeStruct((n, d), F32))
        out_specs.append(pl.BlockSpec((tm, d), lambda i, j: (i, 0)))
    for w, (a, b) in zip(widths, out_ranges):
        out_shape.append(jax.ShapeDtypeStruct((n, w), F32))
        out_specs.append(pl.BlockSpec(
            (tm, tn), lambda i, j, a=a, b=b: (i, jnp.clip(j - a, 0, b - a - 1))))

    return pl.pallas_call(
        functools.partial(_norm_proj_kernel, has_moe=has_moe, out_ranges=tuple(out_ranges)),
        grid=(n // tm, n_chunks),
        in_specs=in_specs,
        out_specs=out_specs,
        out_shape=out_shape,
        scratch_shapes=[pltpu.VMEM((tm, d), F32)],
        compiler_params=_params(("parallel", "arbitrary")),
        name="norm_proj",
    )(*args)


def _final_norm_kernel(x_ref, moe_ref, route_ref, g_ref, y_ref):
    r = route_ref[...]
    x = x_ref[...] + (moe_ref[0] * r[:, 2:3] + moe_ref[1] * r[:, 3:4])
    y_ref[...] = _rms_scale(x) * g_ref[...]


def _final_norm(x, moe, route, g, *, tm):
    n, d = x.shape
    return pl.pallas_call(
        _final_norm_kernel,
        grid=(n // tm,),
        in_specs=[pl.BlockSpec((tm, d), lambda i: (i, 0)),
                  pl.BlockSpec((TOP_K, tm, d), lambda i: (0, i, 0)),
                  pl.BlockSpec((tm, LANES), lambda i: (i, 0)),
                  pl.BlockSpec((1, d), lambda i: (0, 0))],
        out_specs=pl.BlockSpec((tm, d), lambda i: (i, 0)),
        out_shape=jax.ShapeDtypeStruct((n, d), F32),
        compiler_params=_params(("parallel",)),
        name="final_norm",
    )(x, moe, route, g[None, :])


def _retention_kernel(lg_ref, q_ref, k_ref, v_ref, cos_ref, sin_ref, s0_ref, o_ref, s_ref,
                      *, chunk_true, zero_init):
    h = pl.program_id(1)
    c = pl.program_id(2)
    cb = q_ref.shape[1]
    half = RET_DK // 2

    @pl.when(c == 0)
    def _():
        if zero_init:
            s_ref[0, 0] = jnp.zeros(s_ref.shape[2:], F32)
        else:
            s_ref[0, 0] = s0_ref[0, 0]

    lg = lg_ref[h]
    cos = cos_ref[...]
    sin = sin_ref[...]

    def rot(x):
        x1, x2 = x[:, :half], x[:, half:]
        return jnp.concatenate([x1 * cos - x2 * sin, x1 * sin + x2 * cos], axis=-1)

    q = rot(q_ref[0])
    k = rot(k_ref[0]) * (RET_DK ** -0.5)
    v = v_ref[0]

    ii = lax.broadcasted_iota(I32, (cb, cb), 0)
    jj = lax.broadcasted_iota(I32, (cb, cb), 1)
    causal = ii >= jj
    dmask = jnp.where(causal, jnp.exp(jnp.where(causal, (ii - jj).astype(F32), 0.0) * lg), 0.0)
    ri = lax.broadcasted_iota(I32, (cb, 1), 0).astype(F32)
    q_dec = jnp.exp((ri + 1.0) * lg)
    k_dec = jnp.exp((chunk_true - 1.0 - ri) * lg)
    c_dec = jnp.exp(jnp.full((1, RET_DV), float(chunk_true), F32) * lg)

    s = s_ref[0, 0]
    att = _mm(q, k, _NT) * dmask
    o = _mm(att, v) + _mm(q, s) * q_dec
    kd = k * k_dec
    if cb < LANES:
        kd = jnp.concatenate([kd, jnp.zeros((LANES - cb, RET_DK), F32)], axis=0)
        v = jnp.concatenate([v, jnp.zeros((LANES - cb, RET_DV), F32)], axis=0)
    s_ref[0, 0] = s * c_dec + _mm(kd.T, v)
    o_ref[0] = _rms_scale(o)


def _retention(proj, cos, sin, s0, log_gamma, *, batch, seq, chunk_blk, chunk_true):
    nc = seq // chunk_blk
    proj3 = proj.reshape(batch, seq, proj.shape[-1])
    kcol = RET_HEADS
    vcol = 2 * RET_HEADS * RET_DK // RET_DV
    zero_init = s0 is None
    if zero_init:
        s0 = jnp.zeros((1, 1, RET_DK, RET_DV), F32)
        s0_map = lambda b, h, c, lg: (0, 0, 0, 0)
    else:
        s0_map = lambda b, h, c, lg: (b, h, 0, 0)
    grid_spec = pltpu.PrefetchScalarGridSpec(
        num_scalar_prefetch=1,
        grid=(batch, RET_HEADS, nc),
        in_specs=[
            pl.BlockSpec((1, chunk_blk, RET_DK), lambda b, h, c, lg: (b, c, h)),
            pl.BlockSpec((1, chunk_blk, RET_DK), lambda b, h, c, lg: (b, c, kcol + h)),
            pl.BlockSpec((1, chunk_blk, RET_DV), lambda b, h, c, lg: (b, c, vcol + h)),
            pl.BlockSpec((chunk_blk, RET_DK // 2), lambda b, h, c, lg: (c, 0)),
            pl.BlockSpec((chunk_blk, RET_DK // 2), lambda b, h, c, lg: (c, 0)),
            pl.BlockSpec((1, 1, RET_DK, RET_DV), s0_map),
        ],
        out_specs=[
            pl.BlockSpec((1, chunk_blk, RET_DV), lambda b, h, c, lg: (b, c, h)),
            pl.BlockSpec((1, 1, RET_DK, RET_DV), lambda b, h, c, lg: (b, h, 0, 0)),
        ],
    )
    o, s = pl.pallas_call(
        functools.partial(_retention_kernel, chunk_true=chunk_true, zero_init=zero_init),
        grid_spec=grid_spec,
        out_shape=[jax.ShapeDtypeStruct((batch, seq, RET_HEADS * RET_DV), F32),
                   jax.ShapeDtypeStruct((batch, RET_HEADS, RET_DK, RET_DV), F32)],
        compiler_params=_params(("parallel", "parallel", "arbitrary")),
        name="retention",
    )(log_gamma, proj3, proj3, proj3, cos, sin, s0)
    return o.reshape(batch * seq, RET_HEADS * RET_DV), s


def _route(logits):
    lane = lax.broadcasted_iota(I32, logits.shape, 1)
    neg = -jnp.inf
    big = 4 * LANES
    gmask = (lane >= ROUTE_GROUP_LANE) & (lane < ROUTE_GROUP_LANE + N_GROUPS)
    gl = jnp.where(gmask, logits, neg)
    gmax = jnp.max(gl, axis=-1, keepdims=True)
    gsel = jnp.min(jnp.where(gl == gmax, lane, big), axis=-1, keepdims=True) - ROUTE_GROUP_LANE
    g_w = 1.0 / jnp.sum(jnp.where(gmask, jnp.exp(gl - gmax), 0.0), axis=-1, keepdims=True)
    e_lo = gsel * EXPERTS_PER_GROUP
    emask = (lane >= e_lo) & (lane < e_lo + EXPERTS_PER_GROUP)
    el = jnp.where(emask, logits, neg)
    v1 = jnp.max(el, axis=-1, keepdims=True)
    i1 = jnp.min(jnp.where(el == v1, lane, big), axis=-1, keepdims=True)
    el2 = jnp.where(lane == i1, neg, el)
    v2 = jnp.max(el2, axis=-1, keepdims=True)
    i2 = jnp.min(jnp.where(el2 == v2, lane, big), axis=-1, keepdims=True)
    e2 = jnp.exp(v2 - v1)
    inv = 1.0 / (1.0 + e2)
    gate1 = g_w * inv
    gate2 = g_w * (e2 * inv)
    out = jnp.where(lane == 0, i1.astype(F32), 0.0)
    out = jnp.where(lane == 1, i2.astype(F32), out)
    out = jnp.where(lane == 2, gate1, out)
    out = jnp.where(lane == 3, gate2, out)
    return out


def _out_router_kernel(*refs, has_gate):
    it = iter(refs)
    a_ref = next(it)
    gate_ref = next(it) if has_gate else None
    w_ref, x_ref, gf_ref, wr_ref, br_ref, x1_ref, hm_ref, route_ref = it
    a = a_ref[...]
    if has_gate:
        g = gate_ref[...]
        a = a * (g * (1.0 / (1.0 + jnp.exp(-g))))
    x1 = x_ref[...] + _mm(a, w_ref[...])
    x1_ref[...] = x1
    hm = _rms_scale(x1) * gf_ref[...]
    hm_ref[...] = hm
    logits = _mm(hm, wr_ref[...]) + br_ref[...]
    route_ref[...] = _route(logits)


def _out_router(a, gate_src, w, x, g_ffn, wg, bg, we, be, *, tm):
    n, d = x.shape
    ka = a.shape[1]
    has_gate = gate_src is not None
    wr = jnp.zeros((d, LANES), F32).at[:, :N_EXPERTS].set(we).at[:, N_EXPERTS:N_EXPERTS + N_GROUPS].set(wg)
    br = jnp.zeros((1, LANES), F32).at[0, :N_EXPERTS].set(be).at[0, N_EXPERTS:N_EXPERTS + N_GROUPS].set(bg)
    in_specs = [pl.BlockSpec((tm, ka), lambda i: (i, 0))]
    args = [a]
    if has_gate:
        gblk = gate_src.shape[1] // ka - 1
        in_specs.append(pl.BlockSpec((tm, ka), lambda i: (i, gblk)))
        args.append(gate_src)
    in_specs += [pl.BlockSpec((ka, d), lambda i: (0, 0)),
                 pl.BlockSpec((tm, d), lambda i: (i, 0)),
                 pl.BlockSpec((1, d), lambda i: (0, 0)),
                 pl.BlockSpec((d, LANES), lambda i: (0, 0)),
                 pl.BlockSpec((1, LANES), lambda i: (0, 0))]
    args += [w.astype(BF16), x, g_ffn[None, :], wr.astype(BF16), br]
    return pl.pallas_call(
        functools.partial(_out_router_kernel, has_gate=has_gate),
        grid=(n // tm,),
        in_specs=in_specs,
        out_specs=[pl.BlockSpec((tm, d), lambda i: (i, 0)),
                   pl.BlockSpec((tm, d), lambda i: (i, 0)),
                   pl.BlockSpec((tm, LANES), lambda i: (i, 0))],
        out_shape=[jax.ShapeDtypeStruct((n, d), F32),
                   jax.ShapeDtypeStruct((n, d), F32),
                   jax.ShapeDtypeStruct((n, LANES), F32)],
        compiler_params=_params(("parallel",)),
        name="out_router",
    )(*args)


def _moe_kernel(tab_ref, bexp_ref, nvalid_ref, nused_ref, hm_hbm, w1_ref, w3_ref, w2_ref, out_hbm,
                xbuf, obuf, gsem, ssem, w1c, w3c, w2c, *, bm, n_tok):
    i = pl.program_id(0)
    nb_used = nused_ref[0]
    slot = i % 2

    def gather_copy(blk, sl, r):
        tkv = tab_ref[blk * bm + r]
        if n_tok & (n_tok - 1) == 0:
            row = tkv & (n_tok - 1)
        else:
            row = jnp.where(tkv >= n_tok, tkv - n_tok, jnp.maximum(tkv, 0))
        return pltpu.make_async_copy(hm_hbm.at[pl.ds(row, 1)], xbuf.at[sl, pl.ds(r, 1)], gsem.at[sl])

    def scatter_copy(blk, sl, r):
        tkv = tab_ref[blk * bm + r]
        return pltpu.make_async_copy(obuf.at[sl, pl.ds(r, 1)], out_hbm.at[pl.ds(tkv, 1)], ssem.at[sl])

    def for_rows(make, blk, sl, start):
        for r in range(bm):
            cp = make(blk, sl, r)
            if start:
                cp.start()
            else:
                cp.wait()

    def for_valid_rows(make, blk, sl, start):
        def body(r, carry):
            cp = make(blk, sl, r)
            if start:
                cp.start()
            else:
                cp.wait()
            return carry
        lax.fori_loop(0, nvalid_ref[blk], body, 0)

    def scatter_rows(blk, sl, start):
        full = nvalid_ref[blk] == bm

        @pl.when(full)
        def _():
            for_rows(scatter_copy, blk, sl, start)

        @pl.when(jnp.logical_not(full))
        def _():
            for_valid_rows(scatter_copy, blk, sl, start)

    @pl.when(i < nb_used)
    def _():
        @pl.when(i == 0)
        def _():
            for_rows(gather_copy, 0, 0, True)

        for_rows(gather_copy, i, slot, False)
        x = xbuf[slot]
        for_rows(gather_copy, i + 1, 1 - slot, True)

        changed = jnp.logical_or(i == 0, bexp_ref[i] != bexp_ref[jnp.maximum(i - 1, 0)])

        @pl.when(changed)
        def _():
            w1c[...] = w1_ref[0].astype(BF16)
            w3c[...] = w3_ref[0].astype(BF16)
            w2c[...] = w2_ref[0].astype(BF16)

        h1 = _mm(x, w1c[...])
        h3 = _mm(x, w3c[...])
        y = _mm(h1 * (1.0 / (1.0 + jnp.exp(-h1))) * h3, w2c[...])

        @pl.when(i >= 2)
        def _():
            scatter_rows(i - 2, slot, False)

        obuf[slot] = y
        scatter_rows(i, slot, True)

        @pl.when(i == nb_used - 1)
        def _():
            for_rows(gather_copy, i + 1, 1 - slot, False)
            scatter_rows(i, slot, False)

            @pl.when(i >= 1)
            def _():
                scatter_rows(i - 1, 1 - slot, False)


def _dispatch_tables(eid, bm):
    n = eid.shape[0]
    n_assign = eid.size
    e_flat = eid.T.reshape(-1)
    onehot = (e_flat[:, None] == jnp.arange(N_EXPERTS, dtype=I32)[None, :]).astype(I32)
    csum = jnp.cumsum(onehot, axis=0)
    rank = jnp.sum(onehot * csum, axis=1) - 1
    counts = csum[-1]
    padded = ((counts + bm - 1) // bm) * bm
    pend = jnp.cumsum(padded)
    pstart = pend - padded
    dest = jnp.sum(onehot * pstart[None, :], axis=1) + rank
    nb = n_assign // bm + N_EXPERTS
    tab = jnp.full(((nb + 1) * bm,), -1, I32).at[dest].set(jnp.arange(n_assign, dtype=I32))
    nb_used = pend[-1] // bm
    blk = jnp.minimum(jnp.arange(nb, dtype=I32), nb_used - 1)
    bstart = blk * bm
    bexp = jnp.minimum(jnp.sum((pend[None, :] <= bstart[:, None]).astype(I32), axis=1), N_EXPERTS - 1)
    real_end = (pstart + counts)[bexp]
    nvalid = jnp.clip(real_end - bstart, 0, bm).astype(I32)
    return tab, bexp.astype(I32), nvalid, nb_used.reshape(1).astype(I32)


def _moe(hm, route, w1, w3, w2, *, bm):
    n, d = hm.shape
    eid = route[:, :TOP_K].astype(I32)
    tab, bexp, nvalid, nused = _dispatch_tables(eid, bm)
    nb = bexp.shape[0]
    scratch = [pltpu.VMEM((2, bm, d), F32), pltpu.VMEM((2, bm, d), F32),
               pltpu.SemaphoreType.DMA((2,)), pltpu.SemaphoreType.DMA((2,)),
               pltpu.VMEM((d, D_EXPERT), BF16), pltpu.VMEM((d, D_EXPERT), BF16), pltpu.VMEM((D_EXPERT, d), BF16)]
    wmap = lambda i, tab, bexp, nv, nu: (bexp[i], 0, 0)
    grid_spec = pltpu.PrefetchScalarGridSpec(
        num_scalar_prefetch=4,
        grid=(nb,),
        in_specs=[
            pl.BlockSpec(memory_space=pl.ANY),
            pl.BlockSpec((1, d, D_EXPERT), wmap),
            pl.BlockSpec((1, d, D_EXPERT), wmap),
            pl.BlockSpec((1, D_EXPERT, d), wmap),
        ],
        out_specs=pl.BlockSpec(memory_space=pl.ANY),
        scratch_shapes=scratch,
    )
    out = pl.pallas_call(
        functools.partial(_moe_kernel, bm=bm, n_tok=n),
        grid_spec=grid_spec,
        out_shape=jax.ShapeDtypeStruct((TOP_K * n, d), F32),
        compiler_params=_params(("arbitrary",)),
        name="moe_experts",
    )(tab, bexp, nvalid, nused, hm, w1, w3, w2)
    return out.reshape(TOP_K, n, d)


def _sb_local(z, valid, u):
    sp = jnp.maximum(z, 0.0) + jnp.log(1.0 + jnp.exp(-jnp.abs(z)))
    if valid is not None:
        sp = jnp.where(valid, sp, 0.0)
    sp_hi = sp.astype(BF16)
    sp_lo = (sp - sp_hi.astype(F32)).astype(BF16)
    later = (jnp.dot(sp_hi, u, preferred_element_type=F32)
             + jnp.dot(sp_lo, u, preferred_element_type=F32))
    return (z - sp) - later, jnp.sum(sp, axis=-1, keepdims=True)


def _sb_weights(t, valid, cacc):
    a = jnp.exp(t - cacc)
    if valid is not None:
        a = jnp.where(valid, a, 0.0)
    return a


def _strict_lower(n):
    return jnp.asarray(np.tril(np.ones((n, n), np.float32), -1), BF16)


def _sb_prompt_kernel(bias_ref, q_ref, k_ref, v_ref, u_ref, o_ref, *, tq, unroll):
    h = pl.program_id(1)
    i = pl.program_id(2)
    bias = bias_ref[h]
    qs = (q_ref[...] * SB_SCALE).astype(BF16)
    u = u_ref[...]

    def local(kblk, masked):
        kstart = pl.multiple_of(kblk * tq, tq)
        kb = k_ref[pl.ds(kstart, tq), :].astype(BF16)
        vb = v_ref[pl.ds(kstart, tq), :].astype(BF16)
        z = lax.dot_general(qs, kb, _NT, preferred_element_type=F32) + bias
        valid = None
        if masked:
            row = lax.broadcasted_iota(I32, (tq, tq), 0)
            col = lax.broadcasted_iota(I32, (tq, tq), 1)
            valid = col < row
        t, tot = _sb_local(z, valid, u)
        return t, tot, vb, valid

    def combine(parts, acc, cacc):
        for t, tot, vb, valid in parts:
            a = _sb_weights(t, valid, cacc)
            acc = acc + jnp.dot(a.astype(BF16), vb, preferred_element_type=F32)
            cacc = cacc + tot
        return acc, cacc

    carry = (jnp.zeros((tq, SB_HD), F32), jnp.zeros((tq, 1), F32))
    carry = combine([local(i, True)], *carry)

    rem = i % unroll
    carry = lax.fori_loop(0, rem, lambda j, c: combine([local(i - 1 - j, False)], *c), carry)

    def group(g, c):
        right = i - 1 - rem - g * unroll
        return combine([local(right - w, False) for w in range(unroll)], *c)

    acc, _ = lax.fori_loop(0, i // unroll, group, carry)
    o_ref[...] = acc


def _sb_prompt(q, k, v, bias, *, batch, seq, tq=256, unroll=4):
    n = batch * seq
    nq = seq // tq
    grid_spec = pltpu.PrefetchScalarGridSpec(
        num_scalar_prefetch=1,
        grid=(batch, SB_HEADS, nq),
        in_specs=[
            pl.BlockSpec((tq, SB_HD), lambda b, h, i, bias: (b * nq + i, h)),
            pl.BlockSpec((seq, SB_HD), lambda b, h, i, bias: (b, h)),
            pl.BlockSpec((seq, SB_HD), lambda b, h, i, bias: (b, h)),
            pl.BlockSpec((tq, tq), lambda b, h, i, bias: (0, 0)),
        ],
        out_specs=pl.BlockSpec((tq, SB_HD), lambda b, h, i, bias: (b * nq + i, h)),
    )
    return pl.pallas_call(
        functools.partial(_sb_prompt_kernel, tq=tq, unroll=unroll),
        grid_spec=grid_spec,
        out_shape=jax.ShapeDtypeStruct((n, SB_HEADS * SB_HD), F32),
        compiler_params=_params(("parallel", "parallel", "arbitrary")),
        name="sb_prompt",
    )(bias, q, k, v, _strict_lower(tq))


def _sb_sample_kernel(pt_ref, bias_ref, q_ref, kn_ref, vn_ref, u_ref, *rest, n_group):
    page_refs = rest[:2 * n_group]
    o_ref, qh_ref, acc_ref, cacc_ref = rest[2 * n_group:]
    g = pl.program_id(1)
    tp = q_ref.shape[0]
    m = SB_HEADS * tp
    u = u_ref[...]
    row = lax.broadcasted_iota(I32, (m, 1), 0)
    bias_col = jnp.zeros((m, 1), F32)
    for h in range(SB_HEADS):
        bias_col = jnp.where(row // tp == h, bias_ref[h], bias_col)

    def local(k_heads, valid):
        qh = qh_ref[...]
        s = jnp.concatenate(
            [lax.dot_general(qh[h * tp:(h + 1) * tp], k_heads[h].astype(BF16), _NT,
                             preferred_element_type=F32) for h in range(SB_HEADS)], axis=0)
        return _sb_local(s * SB_SCALE + bias_col, valid, u)

    def combine(parts, acc, cacc):
        for (t, tot), v_heads, valid in parts:
            ab = _sb_weights(t, valid, cacc).astype(BF16)
            acc = acc + jnp.concatenate(
                [jnp.dot(ab[h * tp:(h + 1) * tp], v_heads[h].astype(BF16), preferred_element_type=F32)
                 for h in range(SB_HEADS)], axis=0)
            cacc = cacc + tot
        return acc, cacc

    @pl.when(g == 0)
    def _():
        q = q_ref[...]
        qh_ref[...] = jnp.concatenate(
            [q[:, h * SB_HD:(h + 1) * SB_HD] for h in range(SB_HEADS)], axis=0).astype(BF16)
        pad = jnp.zeros((PAGE_SIZE - tp, SB_HD), F32)
        kn = kn_ref[...]
        vn = vn_ref[...]
        heads = lambda a: [jnp.concatenate([a[:, h * SB_HD:(h + 1) * SB_HD], pad], axis=0)
                           for h in range(SB_HEADS)]
        col = lax.broadcasted_iota(I32, (m, PAGE_SIZE), 1)
        qpos = lax.broadcasted_iota(I32, (m, PAGE_SIZE), 0) % tp
        valid = col < qpos
        acc, cacc = combine([(local(heads(kn), valid), heads(vn), valid)],
                            jnp.zeros(acc_ref.shape, F32), jnp.zeros(cacc_ref.shape, F32))
        acc_ref[...] = acc
        cacc_ref[...] = cacc

    def heads_of(ref):
        return [ref[0, pl.ds(h, PAGE_SIZE, stride=SB_HEADS), :] for h in range(SB_HEADS)]

    parts = [(local(heads_of(page_refs[2 * jj]), None), heads_of(page_refs[2 * jj + 1]), None)
             for jj in range(n_group)]
    acc, cacc = combine(parts, acc_ref[...], cacc_ref[...])
    acc_ref[...] = acc
    cacc_ref[...] = cacc

    @pl.when(g == pl.num_programs(1) - 1)
    def _():
        o_ref[...] = jnp.concatenate([acc[h * tp:(h + 1) * tp] for h in range(SB_HEADS)], axis=1)


def _sb_sample(q, k_new, v_new, bias, cache_k, cache_v, page_table, *, n_group=8):
    batch, n_pages = page_table.shape
    tp = q.shape[0] // batch
    n_pool = cache_k.shape[0]
    width = SB_HEADS * SB_HD
    ck = cache_k.reshape(n_pool, PAGE_SIZE * SB_HEADS, SB_HD)
    cv = cache_v.reshape(n_pool, PAGE_SIZE * SB_HEADS, SB_HD)
    m = SB_HEADS * tp

    def page_spec(jj):
        return pl.BlockSpec(
            (1, PAGE_SIZE * SB_HEADS, SB_HD),
            lambda b, g, pt, bias, jj=jj: (pt[b, n_pages - 1 - (g * n_group + jj)], 0, 0))

    row_spec = pl.BlockSpec((tp, width), lambda b, g, pt, bias: (b, 0))
    in_specs = [row_spec, row_spec, row_spec,
                pl.BlockSpec((PAGE_SIZE, PAGE_SIZE), lambda b, g, pt, bias: (0, 0))]
    args = [q, k_new, v_new, _strict_lower(PAGE_SIZE)]
    for jj in range(n_group):
        in_specs += [page_spec(jj), page_spec(jj)]
        args += [ck, cv]
    grid_spec = pltpu.PrefetchScalarGridSpec(
        num_scalar_prefetch=2,
        grid=(batch, n_pages // n_group),
        in_specs=in_specs,
        out_specs=row_spec,
        scratch_shapes=[pltpu.VMEM((m, SB_HD), BF16), pltpu.VMEM((m, SB_HD), F32), pltpu.VMEM((m, 1), F32)],
    )
    return pl.pallas_call(
        functools.partial(_sb_sample_kernel, n_group=n_group),
        grid_spec=grid_spec,
        out_shape=jax.ShapeDtypeStruct(q.shape, F32),
        compiler_params=_params(("parallel", "arbitrary")),
        name="sb_sample",
    )(page_table, bias, *args)


def _rope_tables(pos):
    half = RET_DK // 2
    inv = ROPE_BASE ** (-jnp.arange(half, dtype=F32) / half)
    ang = pos.astype(F32)[:, None] * inv[None, :]
    return jnp.cos(ang), jnp.sin(ang)


def _trunk(x, pos, s0, attend, p, *, batch, seq, chunk_blk, chunk_true, tm, bm):
    log_gamma = jnp.log(1.0 - 2.0 ** (-5.0 - jnp.arange(RET_HEADS, dtype=F32)))
    cos, sin = _rope_tables(pos)

    (proj,) = _norm_proj(x, [p["g_mix"][0]], [p["ret_w_in"][0]], tm=tm)
    o, state = _retention(proj, cos, sin, s0, log_gamma, batch=batch, seq=seq,
                          chunk_blk=chunk_blk, chunk_true=chunk_true)
    x1, hm, route = _out_router(o, proj, p["ret_wo"][0], x, p["g_ffn"][0], p["moe_wg"][0], p["moe_bg"][0],
                                p["moe_we"][0], p["moe_be"][0], tm=tm)
    moe = _moe(hm, route, p["moe_w1"][0], p["moe_w3"][0], p["moe_w2"][0], bm=bm)

    sbw = SB_HEADS * SB_HD
    x2, k_sh, v_sh, q = _norm_proj(
        x1, [p["g_kv"], p["g_kv"], p["g_mix"][1]],
        [p["kv_w"][:, :sbw], p["kv_w"][:, sbw:], p["sb_wq"][0]],
        moe=moe, route=route, tm=tm)
    att = attend(q, k_sh, v_sh, p["sb_bias"][0])
    x3, hm, route = _out_router(att, None, p["sb_wo"][0], x2, p["g_ffn"][1], p["moe_wg"][1], p["moe_bg"][1],
                                p["moe_we"][1], p["moe_be"][1], tm=tm)
    moe = _moe(hm, route, p["moe_w1"][1], p["moe_w3"][1], p["moe_w2"][1], bm=bm)
    y = _final_norm(x3, moe, route, p["g_final"], tm=tm)
    return y, state, k_sh, v_sh


def kernel(x_prompt, x_sample, state_ret, cache_k, cache_v, page_table, g_mix, g_ffn, g_kv, g_final,
           ret_w_in, ret_wo, kv_w, sb_wq, sb_wo, sb_bias, moe_wg, moe_bg, moe_we, moe_be, moe_w1, moe_w3, moe_w2):
    p = dict(g_mix=g_mix, g_ffn=g_ffn, g_kv=g_kv, g_final=g_final, ret_w_in=ret_w_in, ret_wo=ret_wo,
             kv_w=kv_w, sb_wq=sb_wq, sb_wo=sb_wo, sb_bias=sb_bias, moe_wg=moe_wg, moe_bg=moe_bg,
             moe_we=moe_we, moe_be=moe_be, moe_w1=moe_w1, moe_w3=moe_w3, moe_w2=moe_w2)
    bp, tp, d = x_prompt.shape
    bs, ts, _ = x_sample.shape
    past_len = page_table.shape[1] * PAGE_SIZE

    y_p, st_p, k_p, v_p = _trunk(
        x_prompt.reshape(bp * tp, d), jnp.arange(tp), None,
        functools.partial(_sb_prompt, batch=bp, seq=tp), p,
        batch=bp, seq=tp, chunk_blk=RET_CHUNK, chunk_true=RET_CHUNK, tm=512, bm=256)

    tpad = SAMPLE_T_PAD
    xs = jnp.pad(x_sample, ((0, 0), (0, tpad - ts), (0, 0))).reshape(bs * tpad, d)
    attend_s = functools.partial(_sb_sample, cache_k=cache_k, cache_v=cache_v, page_table=page_table)
    y_s, st_s, k_s, v_s = _trunk(
        xs, past_len + jnp.arange(tpad), state_ret[0],
        lambda q, k, v, bias: attend_s(q, k, v, bias), p,
        batch=bs, seq=tpad, chunk_blk=tpad, chunk_true=ts, tm=bs * tpad, bm=16)

    def rows(a, b, t):
        return a.reshape(b, t, SB_HEADS, SB_HD)

    def unpad(a):
        return a.reshape(bs, tpad, -1)[:, :ts]

    return (y_p.reshape(bp, tp, d), unpad(y_s), st_p[None],
            rows(k_p, bp, tp), rows(v_p, bp, tp), st_s[None],
            rows(unpad(k_s), bs, ts), rows(unpad(v_s), bs, ts))
```
